```python
import jax, jax.numpy as jnp
from jax import lax
import numpy as np

D_MODEL = 1024
BATCH = 1
SEQ = 16384
DEPTH = 1
DEC_BATCH = 32
DEC_SEQ = 32
PAST_LEN = 1024

CHUNK = 64
POOL_WIDTH = 512
POOL_WINDOWS = (2, 4, 8, 16)
POOL_GROUPS = len(POOL_WINDOWS)
POOL_GROUP_DIM = POOL_WIDTH // POOL_GROUPS
POOL_STATE = max(POOL_WINDOWS) - 1
FOX_HEADS = 8
HEAD_DIM = 64
FOX_WIDTH = FOX_HEADS * HEAD_DIM
N_BRANCHES = 2
D_IN = POOL_WIDTH + 3 * FOX_WIDTH + FOX_HEADS + N_BRANCHES * D_MODEL
D_FF = -(-8 * D_MODEL // (3 * 256)) * 256
Q_BLOCK = 128
EPS = 1e-6
NEG_INF = -1e30

kernel_name = "hybrid_pool_fox_streaming_step"


def rmsnorm(x, g):
    xf = x.astype(jnp.float32)
    y = xf * lax.rsqrt(jnp.mean(xf * xf, axis=-1, keepdims=True) + EPS)
    return (y * g.astype(jnp.float32)).astype(x.dtype)


def project_in(xn, w_in):
    z = xn @ w_in
    b, t = z.shape[0], z.shape[1]
    idx = [POOL_WIDTH, POOL_WIDTH + FOX_WIDTH, POOL_WIDTH + 2 * FOX_WIDTH,
           POOL_WIDTH + 3 * FOX_WIDTH, POOL_WIDTH + 3 * FOX_WIDTH + FOX_HEADS]
    p, q, k, v, fl, gl = jnp.split(z, idx, axis=-1)
    q = q.reshape(b, t, FOX_HEADS, HEAD_DIM)
    k = k.reshape(b, t, FOX_HEADS, HEAD_DIM)
    v = v.reshape(b, t, FOX_HEADS, HEAD_DIM)
    return p, q, k, v, fl, gl


def pool_mixer(p_ext, pos0, w_group, pool_scale):
    b = p_ext.shape[0]
    t = p_ext.shape[1] - POOL_STATE
    pf = p_ext.astype(jnp.float32)
    cs = jnp.cumsum(pf, axis=1)
    cs = jnp.concatenate([jnp.zeros_like(cs[:, :1]), cs], axis=1)
    end = cs[:, POOL_STATE + 1:]
    pos = pos0 + jnp.arange(t)
    outs = []
    for g, w in enumerate(POOL_WINDOWS):
        lo, hi = g * POOL_GROUP_DIM, (g + 1) * POOL_GROUP_DIM
        start = cs[:, POOL_STATE + 1 - w: POOL_STATE + 1 - w + t, lo:hi]
        cnt = jnp.minimum(w, pos + 1).astype(jnp.float32)[None, :, None]
        outs.append((end[..., lo:hi] - start) / cnt)
    pooled = jnp.concatenate(outs, axis=-1)
    d = (pooled - pf[:, POOL_STATE:]).reshape(b, t, POOL_GROUPS, POOL_GROUP_DIM)
    mixed = jnp.einsum('btgc,gcd->btgd', d, w_group.astype(jnp.float32)).reshape(b, t, POOL_WIDTH)
    return (mixed * pool_scale.astype(jnp.float32)).astype(p_ext.dtype)


def fox_attend(q, k, v, cq, ck, qpos, kpos):
    logits = jnp.einsum('bqhd,bkhd->bhqk', q, k).astype(jnp.float32) * (HEAD_DIM ** -0.5)
    bias = jnp.transpose(cq, (0, 2, 1))[..., :, None] - jnp.transpose(ck, (0, 2, 1))[..., None, :]
    mask = kpos[None, :] <= qpos[:, None]
    logits = jnp.where(mask, logits + bias, NEG_INF)
    probs = jax.nn.softmax(logits, axis=-1)
    return jnp.einsum('bhqk,bkhd->bqhd', probs.astype(v.dtype), v)


def fox_prompt(q, k, v, logf):
    b, s = q.shape[0], q.shape[1]
    c = jnp.cumsum(logf.astype(jnp.float32), axis=1)
    kpos = jnp.arange(s)

    def block(i):
        s0 = i * Q_BLOCK
        qb = lax.dynamic_slice_in_dim(q, s0, Q_BLOCK, axis=1)
        cq = lax.dynamic_slice_in_dim(c, s0, Q_BLOCK, axis=1)
        qpos = s0 + jnp.arange(Q_BLOCK)
        return fox_attend(qb, k, v, cq, c, qpos, kpos)

    out = lax.map(block, jnp.arange(s // Q_BLOCK))
    return jnp.transpose(out, (1, 0, 2, 3, 4)).reshape(b, s, FOX_HEADS, HEAD_DIM)


def fox_sample(q, k, v, logf, cache_k, cache_v, cache_logf):
    past = cache_k.shape[1]
    t = q.shape[1]
    k_all = jnp.concatenate([cache_k.astype(k.dtype), k], axis=1)
    v_all = jnp.concatenate([cache_v.astype(v.dtype), v], axis=1)
    logf_all = jnp.concatenate([cache_logf.astype(jnp.float32), logf.astype(jnp.float32)], axis=1)
    c = jnp.cumsum(logf_all, axis=1)
    qpos = past + jnp.arange(t)
    kpos = jnp.arange(past + t)
    return fox_attend(q, k_all, v_all, c[:, past:], c, qpos, kpos)


def merge_branches(o_pool, o_attn, gate_logits, w_branch_pool, w_branch_attn, b_gate, w_out):
    g = jax.nn.sigmoid(gate_logits + b_gate)
    g_pool, g_attn = jnp.split(g, N_BRANCHES, axis=-1)
    b, t = o_attn.shape[0], o_attn.shape[1]
    merged = g_pool * (o_pool @ w_branch_pool) + g_attn * (o_attn.reshape(b, t, FOX_WIDTH) @ w_branch_attn)
    return merged @ w_out


def swiglu(x, w_gate, w_up, w_down):
    return (jax.nn.silu(x @ w_gate) * (x @ w_up)) @ w_down


def setup_inputs(seed: int = 0) -> dict:
    key = jax.random.key(seed)
    ks = jax.random.split(key, 24)
    f32 = jnp.float32
    nrm = lambda k, shape, scale: jax.random.normal(k, shape, f32) * scale
    return {
        "x_prompt": nrm(ks[0], (BATCH, SEQ, D_MODEL), 1.0),
        "x_sample": nrm(ks[1], (DEC_BATCH, DEC_SEQ, D_MODEL), 1.0),
        "cache_k": nrm(ks[2], (DEPTH, DEC_BATCH, PAST_LEN, FOX_HEADS, HEAD_DIM), 1.0),
        "cache_v": nrm(ks[3], (DEPTH, DEC_BATCH, PAST_LEN, FOX_HEADS, HEAD_DIM), 1.0),
        "cache_logf": jax.nn.log_sigmoid(2.0 + nrm(ks[4], (DEPTH, DEC_BATCH, PAST_LEN, FOX_HEADS), 1.0)),
        "state_pool": nrm(ks[5], (DEPTH, DEC_BATCH, POOL_STATE, POOL_WIDTH), 1.0),
        "norm_mix": 1.0 + nrm(ks[6], (DEPTH, D_MODEL), 0.02),
        "w_in": nrm(ks[7], (DEPTH, D_MODEL, D_IN), D_MODEL ** -0.5),
        "b_forget": 2.0 + nrm(ks[8], (DEPTH, FOX_HEADS), 0.1),
        "w_pool_group": nrm(ks[9], (DEPTH, POOL_GROUPS, POOL_GROUP_DIM, POOL_GROUP_DIM), POOL_GROUP_DIM ** -0.5),
        "pool_scale": 1.0 + nrm(ks[10], (DEPTH, POOL_WIDTH), 0.1),
        "w_branch_pool": nrm(ks[11], (DEPTH, POOL_WIDTH, D_MODEL), POOL_WIDTH ** -0.5),
        "w_branch_attn": nrm(ks[12], (DEPTH, FOX_WIDTH, D_MODEL), FOX_WIDTH ** -0.5),
        "b_gate": nrm(ks[13], (DEPTH, N_BRANCHES * D_MODEL), 0.02),
        "w_out": nrm(ks[14], (DEPTH, D_MODEL, D_MODEL), D_MODEL ** -0.5),
        "norm_ffn": 1.0 + nrm(ks[15], (DEPTH, D_MODEL), 0.02),
        "w_ffn_gate": nrm(ks[16], (DEPTH, D_MODEL, D_FF), D_MODEL ** -0.5),
        "w_ffn_up": nrm(ks[17], (DEPTH, D_MODEL, D_FF), D_MODEL ** -0.5),
        "w_ffn_down": nrm(ks[18], (DEPTH, D_FF, D_MODEL), D_FF ** -0.5),
        "norm_final": 1.0 + nrm(ks[19], (D_MODEL,), 0.02),
    }


def reference(x_prompt, x_sample, cache_k, cache_v, cache_logf, state_pool, norm_mix, w_in, b_forget,
              w_pool_group, pool_scale, w_branch_pool, w_branch_attn, b_gate, w_out, norm_ffn,
              w_ffn_gate, w_ffn_up, w_ffn_down, norm_final):
    xp, xs = x_prompt, x_sample
    kp_l, vp_l, fp_l, pp_l, ks_l, vs_l, fs_l, ps_l = [], [], [], [], [], [], [], []
    for l in range(DEPTH):
        xn = rmsnorm(xp, norm_mix[l])
        p, q, k, v, fl, gl = project_in(xn, w_in[l])
        logf = jax.nn.log_sigmoid((fl + b_forget[l]).astype(jnp.float32))
        p_ext = jnp.concatenate([jnp.zeros((p.shape[0], POOL_STATE, POOL_WIDTH), p.dtype), p], axis=1)
        o_pool = pool_mixer(p_ext, 0, w_pool_group[l], pool_scale[l])
        o_attn = fox_prompt(q, k, v, logf)
        xp = xp + merge_branches(o_pool, o_attn, gl, w_branch_pool[l], w_branch_attn[l], b_gate[l], w_out[l])
        xp = xp + swiglu(rmsnorm(xp, norm_ffn[l]), w_ffn_gate[l], w_ffn_up[l], w_ffn_down[l])
        kp_l.append(k)
        vp_l.append(v)
        fp_l.append(logf)
        pp_l.append(p_ext[:, -POOL_STATE:])

        past = cache_k.shape[2]
        xn = rmsnorm(xs, norm_mix[l])
        p, q, k, v, fl, gl = project_in(xn, w_in[l])
        logf = jax.nn.log_sigmoid((fl + b_forget[l]).astype(jnp.float32))
        p_ext = jnp.concatenate([state_pool[l].astype(p.dtype), p], axis=1)
        o_pool = pool_mixer(p_ext, past, w_pool_group[l], pool_scale[l])
        o_attn = fox_sample(q, k, v, logf, cache_k[l], cache_v[l], cache_logf[l])
        xs = xs + merge_branches(o_pool, o_attn, gl, w_branch_pool[l], w_branch_attn[l], b_gate[l], w_out[l])
        xs = xs + swiglu(rmsnorm(xs, norm_ffn[l]), w_ffn_gate[l], w_ffn_up[l], w_ffn_down[l])
        ks_l.append(k)
        vs_l.append(v)
        fs_l.append(logf)
        ps_l.append(p_ext[:, -POOL_STATE:])

    y_prompt = rmsnorm(xp, norm_final)
    y_sample = rmsnorm(xs, norm_final)
    return (y_prompt, y_sample,
            jnp.stack(kp_l), jnp.stack(vp_l), jnp.stack(fp_l), jnp.stack(pp_l),
            jnp.stack(ks_l), jnp.stack(vs_l), jnp.stack(fs_l), jnp.stack(ps_l))
```

```python
import functools
import math

import jax
import jax.numpy as jnp
from jax import lax
from jax.experimental import pallas as pl
from jax.experimental.pallas import tpu as pltpu

F32 = jnp.float32
BF16 = jnp.bfloat16

D_MODEL = 1024
POOL_WIDTH = 512
POOL_WINDOWS = (2, 4, 8, 16)
POOL_GROUP_DIM = 128
POOL_STATE = 15
HIST = 16
FOX_HEADS = 8
HEAD_DIM = 64
FOX_WIDTH = FOX_HEADS * HEAD_DIM
D_FF = 2816
EPS = 1e-6
NEG_INF = -1e30
LOG2E = math.log2(math.e)
Q_SCALE = HEAD_DIM ** -0.5 * LOG2E

LANES = 128
SLAB = 128
TM_IN = 512
TQ = 256
TM_FFN = 512
VMEM_LIMIT = 56 * 1024 * 1024


def _rms(x, g):
    return x * lax.rsqrt(jnp.mean(x * x, axis=-1, keepdims=True) + EPS) * g


def _log_sigmoid(y):
    return jnp.minimum(y, 0.0) - jnp.log1p(jnp.exp(-jnp.abs(y)))


def _sigmoid(y):
    return 1.0 / (1.0 + jnp.exp(-y))


def _split3(c):
    hi = c.astype(BF16).astype(F32)
    r = c - hi
    mid = r.astype(BF16).astype(F32)
    lo = (r - mid).astype(BF16).astype(F32)
    return hi, mid, lo


def _lane_cumsum(x):
    n = x.shape[-1]
    width = -(-n // LANES) * LANES
    if width != n:
        x = jnp.concatenate([x, jnp.zeros(x.shape[:-1] + (width - n,), x.dtype)], axis=-1)
    lane = lax.broadcasted_iota(jnp.int32, x.shape, x.ndim - 1)
    s = 1
    while s < n:
        x = x + jnp.where(lane >= s, pltpu.roll(x, s, x.ndim - 1), 0.0)
        s *= 2
    return x[..., :n]


def _window_means(ext_ref, t, pos0):
    row = lax.broadcasted_iota(jnp.int32, (t, POOL_GROUP_DIM), 0) + pos0
    outs = []
    for g, w in enumerate(POOL_WINDOWS):
        lo, hi = g * POOL_GROUP_DIM, (g + 1) * POOL_GROUP_DIM
        cur = ext_ref[HIST:HIST + t, lo:hi]
        acc = cur
        for j in range(1, w):
            acc = acc + ext_ref[HIST - j:HIST - j + t, lo:hi]
        cnt = jnp.minimum(w, row + 1).astype(F32)
        outs.append(acc / cnt - cur)
    return outs


def _pool_mix(ext_ref, t, pos0, wg_ref, scale_ref):
    d = _window_means(ext_ref, t, pos0)
    mixed = [jnp.dot(d[g].astype(BF16), wg_ref[g], preferred_element_type=F32)
             for g in range(len(POOL_WINDOWS))]
    return jnp.concatenate(mixed, axis=-1) * scale_ref[...]


def _inproj_prompt_kernel(x_ref, g_ref, w_ref, wfl_ref, wflt_ref, bf_ref, bft_ref, wg_ref, ps_ref,
                          k_ref, v_ref, logf_ref, qat_ref, ka_ref, vat_ref, opool_ref, pstate_ref,
                          ext_ref, carry_ref):
    i = pl.program_id(0)
    tm = x_ref.shape[0]

    @pl.when(i == 0)
    def _():
        ext_ref[0:HIST, :] = jnp.zeros((HIST, POOL_WIDTH), F32)
        carry_ref[...] = jnp.zeros_like(carry_ref)

    xn = _rms(x_ref[...], g_ref[...]).astype(BF16)
    z = jnp.dot(xn, w_ref[...], preferred_element_type=F32)
    p = z[:, 0:POOL_WIDTH]
    zq = z[:, POOL_WIDTH:POOL_WIDTH + FOX_WIDTH]
    zk = z[:, POOL_WIDTH + FOX_WIDTH:POOL_WIDTH + 2 * FOX_WIDTH]
    zv = z[:, POOL_WIDTH + 2 * FOX_WIDTH:POOL_WIDTH + 3 * FOX_WIDTH]
    k_ref[...] = zk
    v_ref[...] = zv

    fl = jnp.dot(xn, wfl_ref[...], preferred_element_type=F32)
    logf_ref[...] = _log_sigmoid(fl[:, 0:FOX_HEADS] + bf_ref[...])
    flt = lax.dot_general(wflt_ref[...], xn, (((1,), (1,)), ((), ())),
                          preferred_element_type=F32)
    logft = _log_sigmoid(flt[0:FOX_HEADS, :] + bft_ref[:, 0:1])
    ct = _lane_cumsum(logft) + carry_ref[:, 0:1]
    carry_ref[...] = jnp.broadcast_to(ct[:, tm - 1:tm], carry_ref.shape)
    chi, cmid, clo = _split3(ct * LOG2E)

    ones24 = jnp.ones((3 * FOX_HEADS, tm), F32)
    zeros16 = jnp.zeros((SLAB // 2 - 6 * FOX_HEADS, tm), F32)
    cq_t = jnp.concatenate([chi, cmid, clo, ones24, zeros16], axis=0)
    ck_t = jnp.concatenate([ones24, -chi, -cmid, -clo, zeros16], axis=0)
    ck2 = jnp.concatenate([ck_t, ck_t], axis=0).T
    lane = lax.broadcasted_iota(jnp.int32, (tm, SLAB), 1)
    low_half = lane < HEAD_DIM
    one_row = (lax.broadcasted_iota(jnp.int32, (SLAB // 2, tm), 0) == 0).astype(F32)

    for j in range(FOX_HEADS // 2):
        zq2 = zq[:, j * SLAB:(j + 1) * SLAB] * Q_SCALE
        zk2 = zk[:, j * SLAB:(j + 1) * SLAB]
        zv2 = zv[:, j * SLAB:(j + 1) * SLAB]
        qt2 = zq2.T
        vt2 = zv2.T
        for e in range(2):
            h = 2 * j + e
            rows = slice(h * SLAB, (h + 1) * SLAB)
            qh = qt2[e * HEAD_DIM:(e + 1) * HEAD_DIM, :]
            vh = vt2[e * HEAD_DIM:(e + 1) * HEAD_DIM, :]
            q_slab = jnp.concatenate([qh, cq_t] if e == 0 else [cq_t, qh], axis=0)
            qat_ref[rows, :] = q_slab.astype(BF16)
            v_slab = jnp.concatenate([vh, one_row], axis=0).astype(BF16)
            for c in range(tm // TQ):
                vat_ref[c, rows, :] = v_slab[:, c * TQ:(c + 1) * TQ]
            bias_lanes = jnp.where((lane & (FOX_HEADS - 1)) == h, ck2, 0.0)
            k_slab = jnp.where(low_half if e == 0 else ~low_half, zk2, bias_lanes)
            ka_ref[:, rows] = k_slab.astype(BF16)

    ext_ref[HIST:HIST + tm, :] = p
    opool_ref[...] = _pool_mix(ext_ref, tm, i * tm, wg_ref, ps_ref).astype(BF16)
    tail = p[tm - HIST:tm, :]
    ext_ref[0:HIST, :] = tail
    pstate_ref[...] = tail


def _inproj_prompt(x, g, w_main, w_fl, w_flt, bf, bft, wg, ps):
    n = x.shape[0]
    tm = min(TM_IN, n)
    nt = n // tm
    const = lambda *shape: pl.BlockSpec(shape, lambda i: (0,) * len(shape))
    return pl.pallas_call(
        _inproj_prompt_kernel,
        grid=(nt,),
        in_specs=[
            pl.BlockSpec((tm, D_MODEL), lambda i: (i, 0)),
            const(1, D_MODEL),
            const(D_MODEL, 4 * FOX_WIDTH),
            const(D_MODEL, LANES),
            const(16, D_MODEL),
            const(1, FOX_HEADS),
            const(FOX_HEADS, LANES),
            const(4, POOL_GROUP_DIM, POOL_GROUP_DIM),
            const(1, POOL_WIDTH),
        ],
        out_specs=[
            pl.BlockSpec((tm, FOX_WIDTH), lambda i: (i, 0)),
            pl.BlockSpec((tm, FOX_WIDTH), lambda i: (i, 0)),
            pl.BlockSpec((tm, FOX_HEADS), lambda i: (i, 0)),
            pl.BlockSpec((FOX_HEADS * SLAB, tm), lambda i: (0, i)),
            pl.BlockSpec((tm, FOX_HEADS * SLAB), lambda i: (i, 0)),
            pl.BlockSpec((tm // TQ, FOX_HEADS * SLAB, TQ), lambda i: (i, 0, 0)),
            pl.BlockSpec((tm, POOL_WIDTH), lambda i: (i, 0)),
            const(HIST, POOL_WIDTH),
        ],
        out_shape=[
            jax.ShapeDtypeStruct((n, FOX_WIDTH), F32),
            jax.ShapeDtypeStruct((n, FOX_WIDTH), F32),
            jax.ShapeDtypeStruct((n, FOX_HEADS), F32),
            jax.ShapeDtypeStruct((FOX_HEADS * SLAB, n), BF16),
            jax.ShapeDtypeStruct((n, FOX_HEADS * SLAB), BF16),
            jax.ShapeDtypeStruct((n // TQ, FOX_HEADS * SLAB, TQ), BF16),
            jax.ShapeDtypeStruct((n, POOL_WIDTH), BF16),
            jax.ShapeDtypeStruct((HIST, POOL_WIDTH), F32),
        ],
        scratch_shapes=[pltpu.VMEM((HIST + tm, POOL_WIDTH), F32), pltpu.VMEM((FOX_HEADS, LANES), F32)],
        compiler_params=pltpu.CompilerParams(dimension_semantics=("arbitrary",),
                                             vmem_limit_bytes=VMEM_LIMIT),
        name="inproj_prompt",
    )(x, g, w_main, w_fl, w_flt, bf, bft, wg, ps)


def _attn_prompt_kernel(qat_ref, ka_ref, vat_ref, o_ref):
    i = pl.program_id(1)
    tq = qat_ref.shape[1]
    qts = [qat_ref[e * SLAB:(e + 1) * SLAB, :] for e in range(2)]

    def step(kb, carry, masked):
        k0 = pl.multiple_of(kb * tq, tq)
        new = []
        for e in range(2):
            m, acc = carry[e]
            kblk = ka_ref[pl.ds(k0, tq), e * SLAB:(e + 1) * SLAB]
            st = jnp.dot(kblk, qts[e], preferred_element_type=F32)
            if masked:
                r = lax.broadcasted_iota(jnp.int32, st.shape, 0)
                c = lax.broadcasted_iota(jnp.int32, st.shape, 1)
                st = jnp.where(r <= c, st, NEG_INF)
            m_new = jnp.maximum(m, jnp.max(st, axis=0, keepdims=True))
            alpha = jnp.exp2(m - m_new)
            pt = jnp.exp2(st - m_new).astype(BF16)
            vblk = vat_ref[kb, e * SLAB:(e + 1) * SLAB, :]
            acc = alpha * acc + jnp.dot(vblk, pt, preferred_element_type=F32)
            new.append((m_new, acc))
        return tuple(new)

    init = tuple((jnp.full((1, tq), NEG_INF, F32), jnp.zeros((SLAB, tq), F32)) for _ in range(2))
    carry = lax.fori_loop(0, i, lambda kb, c: step(kb, c, False), init)
    carry = step(i, carry, True)
    halves = []
    for e in range(2):
        _, acc = carry[e]
        halves.append(acc[0:HEAD_DIM, :] / acc[HEAD_DIM:HEAD_DIM + 1, :])
    o_ref[...] = jnp.concatenate(halves, axis=0).T.astype(BF16)


def _attn_prompt(qat, ka, vat):
    n = ka.shape[0]
    nq = n // TQ
    pairs = FOX_HEADS // 2
    return pl.pallas_call(
        _attn_prompt_kernel,
        grid=(pairs, nq),
        in_specs=[
            pl.BlockSpec((2 * SLAB, TQ), lambda j, i: (j, i)),
            pl.BlockSpec((n, 2 * SLAB), lambda j, i: (0, j), pipeline_mode=pl.Buffered(1)),
            pl.BlockSpec((nq, 2 * SLAB, TQ), lambda j, i: (0, j, 0), pipeline_mode=pl.Buffered(1)),
        ],
        out_specs=pl.BlockSpec((TQ, 2 * HEAD_DIM), lambda j, i: (i, j)),
        out_shape=jax.ShapeDtypeStruct((n, FOX_WIDTH), BF16),
        compiler_params=pltpu.CompilerParams(dimension_semantics=("arbitrary", "arbitrary"),
                                             vmem_limit_bytes=VMEM_LIMIT),
        name="attn_prompt",
    )(qat, ka, vat)


def _inproj_sample_kernel(x_ref, g_ref, w_ref, wfl_ref, wflt_ref, bf_ref, bft_ref,
                          p_ref, q_ref, k_ref, v_ref, logf_ref, logft_ref):
    xn = _rms(x_ref[...], g_ref[...]).astype(BF16)
    z = jnp.dot(xn, w_ref[...], preferred_element_type=F32)
    p_ref[...] = z[:, 0:POOL_WIDTH]
    q_ref[...] = z[:, POOL_WIDTH:POOL_WIDTH + FOX_WIDTH] * Q_SCALE
    k_ref[...] = z[:, POOL_WIDTH + FOX_WIDTH:POOL_WIDTH + 2 * FOX_WIDTH]
    v_ref[...] = z[:, POOL_WIDTH + 2 * FOX_WIDTH:POOL_WIDTH + 3 * FOX_WIDTH]
    fl = jnp.dot(xn, wfl_ref[...], preferred_element_type=F32)
    logf_ref[...] = _log_sigmoid(fl[:, 0:FOX_HEADS] + bf_ref[...])
    flt = lax.dot_general(wflt_ref[...], xn, (((1,), (1,)), ((), ())), preferred_element_type=F32)
    logft_ref[...] = _log_sigmoid(flt[0:FOX_HEADS, :] + bft_ref[:, 0:1])


def _inproj_sample(x, g, w_main, w_fl, w_flt, bf, bft):
    n = x.shape[0]
    tm = min(TM_IN, n)
    const = lambda *shape: pl.BlockSpec(shape, lambda i: (0,) * len(shape))
    row = lambda width: pl.BlockSpec((tm, width), lambda i: (i, 0))
    return pl.pallas_call(
        _inproj_sample_kernel,
        grid=(n // tm,),
        in_specs=[row(D_MODEL), const(1, D_MODEL), const(D_MODEL, 4 * FOX_WIDTH), const(D_MODEL, LANES),
                  const(16, D_MODEL), const(1, FOX_HEADS), const(FOX_HEADS, LANES)],
        out_specs=[row(POOL_WIDTH), row(FOX_WIDTH), row(FOX_WIDTH), row(FOX_WIDTH), row(FOX_HEADS),
                   pl.BlockSpec((FOX_HEADS, tm), lambda i: (0, i))],
        out_shape=[
            jax.ShapeDtypeStruct((n, POOL_WIDTH), F32),
            jax.ShapeDtypeStruct((n, FOX_WIDTH), F32),
            jax.ShapeDtypeStruct((n, FOX_WIDTH), F32),
            jax.ShapeDtypeStruct((n, FOX_WIDTH), F32),
            jax.ShapeDtypeStruct((n, FOX_HEADS), F32),
            jax.ShapeDtypeStruct((FOX_HEADS, n), F32),
        ],
        compiler_params=pltpu.CompilerParams(dimension_semantics=("arbitrary",),
                                             vmem_limit_bytes=VMEM_LIMIT),
        name="inproj_sample",
    )(x, g, w_main, w_fl, w_flt, bf, bft)


def _mix_sample_kernel(p_ref, st_ref, q_ref, k_ref, v_ref, logft_ref, ck_ref, cv_ref, clft_ref, wg_ref, ps_ref,
                       opool_ref, oattn_ref, pstate_ref, ext_ref, *, past):
    t = p_ref.shape[1]
    hq = FOX_HEADS * t

    p = p_ref[0]
    ext_ref[0:1, :] = jnp.zeros((1, POOL_WIDTH), F32)
    ext_ref[HIST - POOL_STATE:HIST, :] = st_ref[0]
    ext_ref[HIST:HIST + t, :] = p
    opool_ref[0] = _pool_mix(ext_ref, t, past, wg_ref, ps_ref).astype(BF16)
    pstate_ref[0] = ext_ref[HIST + t - POOL_STATE:HIST + t, :]

    c_cache = _lane_cumsum(clft_ref[0])
    c_new = _lane_cumsum(logft_ref[0]) + c_cache[:, past - 1:past]
    expand = lambda c: jnp.concatenate(
        [jnp.broadcast_to(c[h:h + 1, :], (t, c.shape[1])) for h in range(FOX_HEADS)], axis=0)
    bias_cache = expand(c_cache) * (-LOG2E)
    bias_new = expand(c_new) * (-LOG2E)

    q = q_ref[0]
    q_rep = jnp.concatenate([q] * FOX_HEADS, axis=0)
    row_head = lax.broadcasted_iota(jnp.int32, (hq, FOX_WIDTH), 0) >> (t.bit_length() - 1)
    lane_head = lax.broadcasted_iota(jnp.int32, (hq, FOX_WIDTH), 1) >> (HEAD_DIM.bit_length() - 1)
    own = row_head == lane_head
    q_bd = jnp.where(own, q_rep, 0.0).astype(BF16)
    nt = (((1,), (1,)), ((), ()))
    s_cache = lax.dot_general(q_bd, ck_ref[0].astype(BF16), nt, preferred_element_type=F32) + bias_cache
    s_new = lax.dot_general(q_bd, k_ref[0].astype(BF16), nt, preferred_element_type=F32) + bias_new
    qpos = lax.broadcasted_iota(jnp.int32, (hq, t), 0) & (t - 1)
    kpos = lax.broadcasted_iota(jnp.int32, (hq, t), 1)
    s_new = jnp.where(kpos <= qpos, s_new, NEG_INF)
    m = jnp.maximum(jnp.max(s_cache, axis=1, keepdims=True), jnp.max(s_new, axis=1, keepdims=True))
    p_cache = jnp.exp2(s_cache - m).astype(BF16)
    p_new = jnp.exp2(s_new - m).astype(BF16)
    denom = (jnp.sum(p_cache.astype(F32), axis=1, keepdims=True)
             + jnp.sum(p_new.astype(F32), axis=1, keepdims=True))
    o_all = (jnp.dot(p_cache, cv_ref[0].astype(BF16), preferred_element_type=F32)
             + jnp.dot(p_new, v_ref[0].astype(BF16), preferred_element_type=F32)) / denom
    o_all = jnp.where(own, o_all, 0.0)
    o = o_all[0:t, :]
    for h in range(1, FOX_HEADS):
        o = o + o_all[h * t:(h + 1) * t, :]
    oattn_ref[0] = o.astype(BF16)


def _mix_sample(p, state, q, k, v, logft, cache_k, cache_v, cache_logft, wg, ps):
    b, t, _ = p.shape
    past = cache_k.shape[1]
    per = lambda *shape: pl.BlockSpec((1,) + shape, lambda i: (i,) + (0,) * len(shape))
    const = lambda *shape: pl.BlockSpec(shape, lambda i: (0,) * len(shape))
    return pl.pallas_call(
        functools.partial(_mix_sample_kernel, past=past),
        grid=(b,),
        in_specs=[per(t, POOL_WIDTH), per(POOL_STATE, POOL_WIDTH), per(t, FOX_WIDTH), per(t, FOX_WIDTH),
                  per(t, FOX_WIDTH), per(FOX_HEADS, t), per(past, FOX_WIDTH), per(past, FOX_WIDTH),
                  per(FOX_HEADS, past), const(4, POOL_GROUP_DIM, POOL_GROUP_DIM), const(1, POOL_WIDTH)],
        out_specs=[per(t, POOL_WIDTH), per(t, FOX_WIDTH), per(POOL_STATE, POOL_WIDTH)],
        out_shape=[
            jax.ShapeDtypeStruct((b, t, POOL_WIDTH), BF16),
            jax.ShapeDtypeStruct((b, t, FOX_WIDTH), BF16),
            jax.ShapeDtypeStruct((b, POOL_STATE, POOL_WIDTH), F32),
        ],
        scratch_shapes=[pltpu.VMEM((HIST + t, POOL_WIDTH), F32)],
        compiler_params=pltpu.CompilerParams(dimension_semantics=("arbitrary",),
                                             vmem_limit_bytes=VMEM_LIMIT),
        name="mix_sample",
    )(p, state, q, k, v, logft, cache_k, cache_v, cache_logft, wg, ps)


def _merge_ffn_kernel(x_ref, op_ref, oa_ref, gmix_ref, wgl_ref, bg_ref, wbp_ref, wba_ref, wout_ref,
                      gffn_ref, wgate_ref, wup_ref, wdown_ref, gfin_ref, y_ref):
    x = x_ref[...]
    xn = _rms(x, gmix_ref[...]).astype(BF16)
    gates = _sigmoid(jnp.dot(xn, wgl_ref[...], preferred_element_type=F32) + bg_ref[...])
    a = jnp.dot(op_ref[...], wbp_ref[...], preferred_element_type=F32)
    b = jnp.dot(oa_ref[...], wba_ref[...], preferred_element_type=F32)
    merged = gates[:, 0:D_MODEL] * a + gates[:, D_MODEL:2 * D_MODEL] * b
    h = x + jnp.dot(merged.astype(BF16), wout_ref[...], preferred_element_type=F32)
    hn = _rms(h, gffn_ref[...]).astype(BF16)
    half = D_FF // 2
    ffn = None
    for c in range(2):
        cols = slice(c * half, (c + 1) * half)
        gate = jnp.dot(hn, wgate_ref[:, cols], preferred_element_type=F32)
        up = jnp.dot(hn, wup_ref[:, cols], preferred_element_type=F32)
        act = (gate * _sigmoid(gate) * up).astype(BF16)
        part = jnp.dot(act, wdown_ref[cols, :], preferred_element_type=F32)
        ffn = part if ffn is None else ffn + part
    y_ref[...] = _rms(h + ffn, gfin_ref[...])


def _merge_ffn(x, o_pool, o_attn, gmix, w_gl, b_gate, w_bp, w_ba, w_out, gffn, w_gate, w_up, w_down, gfin):
    n = x.shape[0]
    tm = min(TM_FFN, n)
    row = lambda width: pl.BlockSpec((tm, width), lambda i: (i, 0))
    const = lambda *shape: pl.BlockSpec(shape, lambda i: (0,) * len(shape), pipeline_mode=pl.Buffered(1))
    return pl.pallas_call(
        _merge_ffn_kernel,
        grid=(n // tm,),
        in_specs=[row(D_MODEL), row(POOL_WIDTH), row(FOX_WIDTH), const(1, D_MODEL),
                  const(D_MODEL, 2 * D_MODEL), const(1, 2 * D_MODEL), const(POOL_WIDTH, D_MODEL),
                  const(FOX_WIDTH, D_MODEL), const(D_MODEL, D_MODEL), const(1, D_MODEL),
                  const(D_MODEL, D_FF), const(D_MODEL, D_FF), const(D_FF, D_MODEL), const(1, D_MODEL)],
        out_specs=row(D_MODEL),
        out_shape=jax.ShapeDtypeStruct((n, D_MODEL), F32),
        compiler_params=pltpu.CompilerParams(dimension_semantics=("arbitrary",),
                                             vmem_limit_bytes=VMEM_LIMIT),
        name="merge_ffn",
    )(x, o_pool, o_attn, gmix, w_gl, b_gate, w_bp, w_ba, w_out, gffn, w_gate, w_up, w_down, gfin)


def kernel(x_prompt, x_sample, cache_k, cache_v, cache_logf, state_pool, norm_mix, w_in, b_forget,
           w_pool_group, pool_scale, w_branch_pool, w_branch_attn, b_gate, w_out, norm_ffn,
           w_ffn_gate, w_ffn_up, w_ffn_down, norm_final):
    depth = w_in.shape[0]
    assert depth == 1, "single-layer trunk"
    bp, sp, _ = x_prompt.shape
    bs, ts, _ = x_sample.shape
    assert bp == 1
    past = cache_k.shape[2]
    l = 0

    w_in_b = w_in[l].astype(BF16)
    n_main = POOL_WIDTH + 3 * FOX_WIDTH
    w_main = w_in_b[:, :n_main]
    w_fl8 = w_in_b[:, n_main:n_main + FOX_HEADS]
    w_fl = jnp.pad(w_fl8, ((0, 0), (0, LANES - FOX_HEADS)))
    w_flt = jnp.pad(w_fl8.T, ((0, 16 - FOX_HEADS), (0, 0)))
    w_gl = w_in_b[:, n_main + FOX_HEADS:]
    bf = b_forget[l].reshape(1, FOX_HEADS)
    bft = jnp.broadcast_to(b_forget[l].reshape(FOX_HEADS, 1), (FOX_HEADS, LANES))
    gmix = norm_mix[l].reshape(1, D_MODEL)
    gffn = norm_ffn[l].reshape(1, D_MODEL)
    gfin = norm_final.reshape(1, D_MODEL)
    wg = w_pool_group[l].astype(BF16)
    ps = pool_scale[l].reshape(1, POOL_WIDTH)
    tail = (gmix, w_gl, b_gate[l].reshape(1, 2 * D_MODEL), w_branch_pool[l].astype(BF16),
            w_branch_attn[l].astype(BF16), w_out[l].astype(BF16), gffn, w_ffn_gate[l].astype(BF16),
            w_ffn_up[l].astype(BF16), w_ffn_down[l].astype(BF16), gfin)

    xp = x_prompt.reshape(sp, D_MODEL)
    k_p, v_p, logf_p, qat, ka, vat, opool_p, pstate_p = _inproj_prompt(
        xp, gmix, w_main, w_fl, w_flt, bf, bft, wg, ps)
    oattn_p = _attn_prompt(qat, ka, vat)
    y_p = _merge_ffn(xp, opool_p, oattn_p, *tail)

    xs = x_sample.reshape(bs * ts, D_MODEL)
    p_s, q_s, k_s, v_s, logf_s, logft_s = _inproj_sample(xs, gmix, w_main, w_fl, w_flt, bf, bft)
    logft_s3 = logft_s.reshape(FOX_HEADS, bs, ts).transpose(1, 0, 2)
    clft = cache_logf[l].transpose(0, 2, 1)
    opool_s, oattn_s, pstate_s = _mix_sample(
        p_s.reshape(bs, ts, POOL_WIDTH), state_pool[l], q_s.reshape(bs, ts, FOX_WIDTH),
        k_s.reshape(bs, ts, FOX_WIDTH), v_s.reshape(bs, ts, FOX_WIDTH), logft_s3,
        cache_k[l].reshape(bs, past, FOX_WIDTH), cache_v[l].reshape(bs, past, FOX_WIDTH), clft, wg, ps)
    y_s = _merge_ffn(xs, opool_s.reshape(bs * ts, POOL_WIDTH), oattn_s.reshape(bs * ts, FOX_WIDTH), *tail)

    return (y_p.reshape(bp, sp, D_MODEL), y_s.reshape(bs, ts, D_MODEL),
            k_p.reshape(1, bp, sp, FOX_HEADS, HEAD_DIM), v_p.reshape(1, bp, sp, FOX_HEADS, HEAD_DIM),
            logf_p.reshape(1, bp, sp, FOX_HEADS), pstate_p[HIST - POOL_STATE:].reshape(1, bp, POOL_STATE, POOL_WIDTH),
            k_s.reshape(1, bs, ts, FOX_HEADS, HEAD_DIM), v_s.reshape(1, bs, ts, FOX_HEADS, HEAD_DIM),
            logf_s.reshape(1, bs, ts, FOX_HEADS), pstate_s.reshape(1, bs, POOL_STATE, POOL_WIDTH))
```

```python
import functools
import math

import jax
import jax.numpy as jnp
from jax import lax
from jax.experimental import pallas as pl
from jax.experimental.pallas import tpu as pltpu

F32 = jnp.float32
BF16 = jnp.bfloat16

D_MODEL = 1024
POOL_WIDTH = 512
POOL_WINDOWS = (2, 4, 8, 16)
POOL_GROUP_DIM = 128
POOL_STATE = 15
HIST = 16
FOX_HEADS = 8
HEAD_DIM = 64
FOX_WIDTH = FOX_HEADS * HEAD_DIM
D_FF = 2816
EPS = 1e-6
NEG_INF = -1e30
LOG2E = math.log2(math.e)
Q_SCALE = HEAD_DIM ** -0.5 * LOG2E

LANES = 128
SLAB = 128
TM_IN = 512
TQ = 256
TK = 1024
QB = 2048
TM_FFN = 512
VMEM_LIMIT = 56 * 1024 * 1024


def _rms(x, g):
    return x * lax.rsqrt(jnp.mean(x * x, axis=-1, keepdims=True) + EPS) * g


def _log_sigmoid(y):
    return jnp.minimum(y, 0.0) - jnp.log1p(jnp.exp(-jnp.abs(y)))


def _sigmoid(y):
    return 1.0 / (1.0 + jnp.exp(-y))


def _split3(c):
    hi = c.astype(BF16).astype(F32)
    r = c - hi
    mid = r.astype(BF16).astype(F32)
    lo = (r - mid).astype(BF16).astype(F32)
    return hi, mid, lo


def _lane_cumsum(x):
    n = x.shape[-1]
    width = -(-n // LANES) * LANES
    if width != n:
        x = jnp.concatenate([x, jnp.zeros(x.shape[:-1] + (width - n,), x.dtype)], axis=-1)
    lane = lax.broadcasted_iota(jnp.int32, x.shape, x.ndim - 1)
    s = 1
    while s < n:
        x = x + jnp.where(lane >= s, pltpu.roll(x, s, x.ndim - 1), 0.0)
        s *= 2
    return x[..., :n]


def _window_means(ext_ref, t, pos0):
    row = lax.broadcasted_iota(jnp.int32, (t, POOL_GROUP_DIM), 0) + pos0
    outs = []
    for g, w in enumerate(POOL_WINDOWS):
        lo, hi = g * POOL_GROUP_DIM, (g + 1) * POOL_GROUP_DIM
        cur = ext_ref[HIST:HIST + t, lo:hi]
        acc = cur
        for j in range(1, w):
            acc = acc + ext_ref[HIST - j:HIST - j + t, lo:hi]
        cnt = jnp.minimum(w, row + 1).astype(F32)
        outs.append(acc / cnt - cur)
    return outs


def _pool_mix(ext_ref, t, pos0, wg_ref, scale_ref):
    d = _window_means(ext_ref, t, pos0)
    mixed = [jnp.dot(d[g].astype(BF16), wg_ref[g], preferred_element_type=F32)
             for g in range(len(POOL_WINDOWS))]
    return jnp.concatenate(mixed, axis=-1) * scale_ref[...]


def _inproj_prompt_kernel(x_ref, g_ref, w_ref, wfl_ref, wflt_ref, bf_ref, bft_ref, wg_ref, ps_ref,
                          k_ref, v_ref, logf_ref, qat_ref, ka_ref, vat_ref, opool_ref, pstate_ref,
                          ext_ref, carry_ref):
    i = pl.program_id(0)
    tm = x_ref.shape[0]

    @pl.when(i == 0)
    def _():
        ext_ref[0:HIST, :] = jnp.zeros((HIST, POOL_WIDTH), F32)
        carry_ref[...] = jnp.zeros_like(carry_ref)

    xn = _rms(x_ref[...], g_ref[...]).astype(BF16)
    z = jnp.dot(xn, w_ref[...], preferred_element_type=F32)
    p = z[:, 0:POOL_WIDTH]
    zq = z[:, POOL_WIDTH:POOL_WIDTH + FOX_WIDTH]
    zk = z[:, POOL_WIDTH + FOX_WIDTH:POOL_WIDTH + 2 * FOX_WIDTH]
    zv = z[:, POOL_WIDTH + 2 * FOX_WIDTH:POOL_WIDTH + 3 * FOX_WIDTH]
    k_ref[...] = zk
    v_ref[...] = zv

    fl = jnp.dot(xn, wfl_ref[...], preferred_element_type=F32)
    logf_ref[...] = _log_sigmoid(fl[:, 0:FOX_HEADS] + bf_ref[...])
    flt = lax.dot_general(wflt_ref[...], xn, (((1,), (1,)), ((), ())),
                          preferred_element_type=F32)
    logft = _log_sigmoid(flt[0:FOX_HEADS, :] + bft_ref[:, 0:1])
    ct = _lane_cumsum(logft) + carry_ref[:, 0:1]
    carry_ref[...] = jnp.broadcast_to(ct[:, tm - 1:tm], carry_ref.shape)
    chi, cmid, clo = _split3(ct * LOG2E)

    ones24 = jnp.ones((3 * FOX_HEADS, tm), F32)
    zeros16 = jnp.zeros((SLAB // 2 - 6 * FOX_HEADS, tm), F32)
    cq_t = jnp.concatenate([chi, cmid, clo, ones24, zeros16], axis=0)
    ck_t = jnp.concatenate([ones24, -chi, -cmid, -clo, zeros16], axis=0)
    ck2 = jnp.concatenate([ck_t, ck_t], axis=0).T
    lane = lax.broadcasted_iota(jnp.int32, (tm, SLAB), 1)
    low_half = lane < HEAD_DIM
    one_row = (lax.broadcasted_iota(jnp.int32, (SLAB // 2, tm), 0) == 0).astype(F32)

    for j in range(FOX_HEADS // 2):
        zq2 = zq[:, j * SLAB:(j + 1) * SLAB] * Q_SCALE
        zk2 = zk[:, j * SLAB:(j + 1) * SLAB]
        zv2 = zv[:, j * SLAB:(j + 1) * SLAB]
        qt2 = zq2.T
        vt2 = zv2.T
        for e in range(2):
            h = 2 * j + e
            rows = slice(h * SLAB, (h + 1) * SLAB)
            qh = qt2[e * HEAD_DIM:(e + 1) * HEAD_DIM, :]
            vh = vt2[e * HEAD_DIM:(e + 1) * HEAD_DIM, :]
            q_slab = jnp.concatenate([qh, cq_t] if e == 0 else [cq_t, qh], axis=0)
            qat_ref[rows, :] = q_slab.astype(BF16)
            v_slab = jnp.concatenate([vh, one_row], axis=0).astype(BF16)
            vat_ref[0, rows, :] = v_slab
            bias_lanes = jnp.where((lane & (FOX_HEADS - 1)) == h, ck2, 0.0)
            k_slab = jnp.where(low_half if e == 0 else ~low_half, zk2, bias_lanes)
            ka_ref[:, rows] = k_slab.astype(BF16)

    ext_ref[HIST:HIST + tm, :] = p
    opool_ref[...] = _pool_mix(ext_ref, tm, i * tm, wg_ref, ps_ref).astype(BF16)
    tail = p[tm - HIST:tm, :]
    ext_ref[0:HIST, :] = tail
    pstate_ref[...] = tail


def _inproj_prompt(x, g, w_main, w_fl, w_flt, bf, bft, wg, ps):
    n = x.shape[0]
    tm = min(TM_IN, n)
    nt = n // tm
    tk = min(TK, n)
    const = lambda *shape: pl.BlockSpec(shape, lambda i: (0,) * len(shape))
    return pl.pallas_call(
        _inproj_prompt_kernel,
        grid=(nt,),
        in_specs=[
            pl.BlockSpec((tm, D_MODEL), lambda i: (i, 0)),
            const(1, D_MODEL),
            const(D_MODEL, 4 * FOX_WIDTH),
            const(D_MODEL, LANES),
            const(16, D_MODEL),
            const(1, FOX_HEADS),
            const(FOX_HEADS, LANES),
            const(4, POOL_GROUP_DIM, POOL_GROUP_DIM),
            const(1, POOL_WIDTH),
        ],
        out_specs=[
            pl.BlockSpec((tm, FOX_WIDTH), lambda i: (i, 0)),
            pl.BlockSpec((tm, FOX_WIDTH), lambda i: (i, 0)),
            pl.BlockSpec((tm, FOX_HEADS), lambda i: (i, 0)),
            pl.BlockSpec((FOX_HEADS * SLAB, tm), lambda i: (0, i)),
            pl.BlockSpec((tm, FOX_HEADS * SLAB), lambda i: (i, 0)),
            pl.BlockSpec((1, FOX_HEADS * SLAB, tm), lambda i: (i // (tk // tm), 0, i % (tk // tm))),
            pl.BlockSpec((tm, POOL_WIDTH), lambda i: (i, 0)),
            const(HIST, POOL_WIDTH),
        ],
        out_shape=[
            jax.ShapeDtypeStruct((n, FOX_WIDTH), F32),
            jax.ShapeDtypeStruct((n, FOX_WIDTH), F32),
            jax.ShapeDtypeStruct((n, FOX_HEADS), F32),
            jax.ShapeDtypeStruct((FOX_HEADS * SLAB, n), BF16),
            jax.ShapeDtypeStruct((n, FOX_HEADS * SLAB), BF16),
            jax.ShapeDtypeStruct((n // tk, FOX_HEADS * SLAB, tk), BF16),
            jax.ShapeDtypeStruct((n, POOL_WIDTH), BF16),
            jax.ShapeDtypeStruct((HIST, POOL_WIDTH), F32),
        ],
        scratch_shapes=[pltpu.VMEM((HIST + tm, POOL_WIDTH), F32), pltpu.VMEM((FOX_HEADS, LANES), F32)],
        compiler_params=pltpu.CompilerParams(dimension_semantics=("arbitrary",),
                                             vmem_limit_bytes=VMEM_LIMIT),
        name="inproj_prompt",
    )(x, g, w_main, w_fl, w_flt, bf, bft, wg, ps)


def _attn_prompt_kernel(qat_ref, ka_ref, vat_ref, o_ref, m_ref, acc_ref):
    i = pl.program_id(1)
    t = TQ
    nsub = qat_ref.shape[1] // t
    tk = vat_ref.shape[2]
    per = qat_ref.shape[1] // tk

    for e in range(2):
        for s in range(nsub):
            m_ref[e, s] = jnp.full((8, t), NEG_INF, F32)
            acc_ref[e, s] = jnp.zeros((SLAB, t), F32)

    def stage_qk(g, e, s, nkeys, masked, off):
        k0 = pl.multiple_of(g * tk, tk)
        kblk = ka_ref[pl.ds(k0, nkeys), e * SLAB:(e + 1) * SLAB]
        qt = qat_ref[e * SLAB:(e + 1) * SLAB, s * t:(s + 1) * t]
        st = jnp.dot(kblk, qt, preferred_element_type=F32)
        if masked:
            r = lax.broadcasted_iota(jnp.int32, st.shape, 0)
            c = lax.broadcasted_iota(jnp.int32, st.shape, 1)
            st = jnp.where(r <= c + (s * t - off), st, NEG_INF)
        return st

    def stage_softmax(st, e, s):
        m_old = m_ref[e, s][0:1, :]
        m_new = jnp.maximum(m_old, jnp.max(st, axis=0, keepdims=True))
        alpha = jnp.exp2(m_old - m_new)
        pt = jnp.exp2(st - m_new).astype(BF16)
        m_ref[e, s] = jnp.broadcast_to(m_new, (8, t))
        return alpha, pt

    def stage_pv(g, e, s, nkeys, alpha, pt):
        vblk = vat_ref[g, e * SLAB:(e + 1) * SLAB, 0:nkeys]
        acc_ref[e, s] = alpha * acc_ref[e, s] + jnp.dot(vblk, pt, preferred_element_type=F32)

    def run_chains(chains):
        n = len(chains)
        scores, probs = {}, {}
        for step in range(n + 2):
            if step < n:
                scores[step] = stage_qk(*chains[step])
            if 0 <= step - 1 < n:
                _, e, s = chains[step - 1][:3]
                probs[step - 1] = stage_softmax(scores.pop(step - 1), e, s)
            if 0 <= step - 2 < n:
                g, e, s, nkeys = chains[step - 2][:4]
                stage_pv(g, e, s, nkeys, *probs.pop(step - 2))

    def body(g, carry):
        run_chains([(g, e, s, tk, False, 0) for s in range(nsub) for e in range(2)])
        return carry

    lax.fori_loop(0, per * i, body, 0)

    diag = []
    for u in range(per):
        for s in range(nsub):
            nkeys = min((s + 1) * t - u * tk, tk)
            if nkeys > 0:
                for e in range(2):
                    diag.append((per * i + u, e, s, nkeys, (s + 1) * t <= (u + 1) * tk, u * tk))
    run_chains(diag)

    for s in range(nsub):
        halves = []
        for e in range(2):
            acc = acc_ref[e, s]
            halves.append(acc[0:HEAD_DIM, :] / acc[HEAD_DIM:HEAD_DIM + 1, :])
        o_ref[s * t:(s + 1) * t, :] = jnp.concatenate(halves, axis=0).T.astype(BF16)


def _attn_prompt(qat, ka, vat):
    n = ka.shape[0]
    qb = min(QB, n)
    tk = vat.shape[2]
    nsub = qb // TQ
    pairs = FOX_HEADS // 2
    return pl.pallas_call(
        _attn_prompt_kernel,
        grid=(pairs, n // qb),
        in_specs=[
            pl.BlockSpec((2 * SLAB, qb), lambda j, i: (j, i)),
            pl.BlockSpec((n, 2 * SLAB), lambda j, i: (0, j), pipeline_mode=pl.Buffered(1)),
            pl.BlockSpec((n // tk, 2 * SLAB, tk), lambda j, i: (0, j, 0), pipeline_mode=pl.Buffered(1)),
        ],
        out_specs=pl.BlockSpec((qb, 2 * HEAD_DIM), lambda j, i: (i, j)),
        out_shape=jax.ShapeDtypeStruct((n, FOX_WIDTH), BF16),
        scratch_shapes=[pltpu.VMEM((2, nsub, 8, TQ), F32), pltpu.VMEM((2, nsub, SLAB, TQ), F32)],
        compiler_params=pltpu.CompilerParams(dimension_semantics=("arbitrary", "arbitrary"),
                                             vmem_limit_bytes=VMEM_LIMIT),
        name="attn_prompt",
    )(qat, ka, vat)


def _inproj_sample_kernel(x_ref, g_ref, w_ref, wfl_ref, wflt_ref, bf_ref, bft_ref,
                          p_ref, q_ref, k_ref, v_ref, logf_ref, logft_ref):
    xn = _rms(x_ref[...], g_ref[...]).astype(BF16)
    z = jnp.dot(xn, w_ref[...], preferred_element_type=F32)
    p_ref[...] = z[:, 0:POOL_WIDTH]
    q_ref[...] = z[:, POOL_WIDTH:POOL_WIDTH + FOX_WIDTH] * Q_SCALE
    k_ref[...] = z[:, POOL_WIDTH + FOX_WIDTH:POOL_WIDTH + 2 * FOX_WIDTH]
    v_ref[...] = z[:, POOL_WIDTH + 2 * FOX_WIDTH:POOL_WIDTH + 3 * FOX_WIDTH]
    fl = jnp.dot(xn, wfl_ref[...], preferred_element_type=F32)
    logf_ref[...] = _log_sigmoid(fl[:, 0:FOX_HEADS] + bf_ref[...])
    flt = lax.dot_general(wflt_ref[...], xn, (((1,), (1,)), ((), ())), preferred_element_type=F32)
    logft_ref[...] = _log_sigmoid(flt[0:FOX_HEADS, :] + bft_ref[:, 0:1])


def _inproj_sample(x, g, w_main, w_fl, w_flt, bf, bft):
    n = x.shape[0]
    tm = min(TM_IN, n)
    const = lambda *shape: pl.BlockSpec(shape, lambda i: (0,) * len(shape))
    row = lambda width: pl.BlockSpec((tm, width), lambda i: (i, 0))
    return pl.pallas_call(
        _inproj_sample_kernel,
        grid=(n // tm,),
        in_specs=[row(D_MODEL), const(1, D_MODEL), const(D_MODEL, 4 * FOX_WIDTH), const(D_MODEL, LANES),
                  const(16, D_MODEL), const(1, FOX_HEADS), const(FOX_HEADS, LANES)],
        out_specs=[row(POOL_WIDTH), row(FOX_WIDTH), row(FOX_WIDTH), row(FOX_WIDTH), row(FOX_HEADS),
                   pl.BlockSpec((FOX_HEADS, tm), lambda i: (0, i))],
        out_shape=[
            jax.ShapeDtypeStruct((n, POOL_WIDTH), F32),
            jax.ShapeDtypeStruct((n, FOX_WIDTH), F32),
            jax.ShapeDtypeStruct((n, FOX_WIDTH), F32),
            jax.ShapeDtypeStruct((n, FOX_WIDTH), F32),
            jax.ShapeDtypeStruct((n, FOX_HEADS), F32),
            jax.ShapeDtypeStruct((FOX_HEADS, n), F32),
        ],
        compiler_params=pltpu.CompilerParams(dimension_semantics=("arbitrary",),
                                             vmem_limit_bytes=VMEM_LIMIT),
        name="inproj_sample",
    )(x, g, w_main, w_fl, w_flt, bf, bft)


def _mix_sample_kernel(p_ref, st_ref, q_ref, k_ref, v_ref, logft_ref, ck_ref, cv_ref, clft_ref, wg_ref, ps_ref,
                       opool_ref, oattn_ref, pstate_ref, ext_ref, *, past):
    t = p_ref.shape[1]
    hq = FOX_HEADS * t

    p = p_ref[0]
    ext_ref[0:1, :] = jnp.zeros((1, POOL_WIDTH), F32)
    ext_ref[HIST - POOL_STATE:HIST, :] = st_ref[0]
    ext_ref[HIST:HIST + t, :] = p
    opool_ref[0] = _pool_mix(ext_ref, t, past, wg_ref, ps_ref).astype(BF16)
    pstate_ref[0] = ext_ref[HIST + t - POOL_STATE:HIST + t, :]

    c_cache = _lane_cumsum(clft_ref[0])
    c_new = _lane_cumsum(logft_ref[0]) + c_cache[:, past - 1:past]
    expand = lambda c: jnp.concatenate(
        [jnp.broadcast_to(c[h:h + 1, :], (t, c.shape[1])) for h in range(FOX_HEADS)], axis=0)
    bias_cache = expand(c_cache) * (-LOG2E)
    bias_new = expand(c_new) * (-LOG2E)

    q = q_ref[0]
    q_rep = jnp.concatenate([q] * FOX_HEADS, axis=0)
    row_head = lax.broadcasted_iota(jnp.int32, (hq, FOX_WIDTH), 0) >> (t.bit_length() - 1)
    lane_head = lax.broadcasted_iota(jnp.int32, (hq, FOX_WIDTH), 1) >> (HEAD_DIM.bit_length() - 1)
    own = row_head == lane_head
    q_bd = jnp.where(own, q_rep, 0.0).astype(BF16)
    nt = (((1,), (1,)), ((), ()))
    s_cache = lax.dot_general(q_bd, ck_ref[0].astype(BF16), nt, preferred_element_type=F32) + bias_cache
    s_new = lax.dot_general(q_bd, k_ref[0].astype(BF16), nt, preferred_element_type=F32) + bias_new
    qpos = lax.broadcasted_iota(jnp.int32, (hq, t), 0) & (t - 1)
    kpos = lax.broadcasted_iota(jnp.int32, (hq, t), 1)
    s_new = jnp.where(kpos <= qpos, s_new, NEG_INF)
    m = jnp.maximum(jnp.max(s_cache, axis=1, keepdims=True), jnp.max(s_new, axis=1, keepdims=True))
    p_cache = jnp.exp2(s_cache - m).astype(BF16)
    p_new = jnp.exp2(s_new - m).astype(BF16)
    denom = (jnp.sum(p_cache.astype(F32), axis=1, keepdims=True)
             + jnp.sum(p_new.astype(F32), axis=1, keepdims=True))
    o_all = (jnp.dot(p_cache, cv_ref[0].astype(BF16), preferred_element_type=F32)
             + jnp.dot(p_new, v_ref[0].astype(BF16), preferred_element_type=F32)) / denom
    o_all = jnp.where(own, o_all, 0.0)
    o = o_all[0:t, :]
    for h in range(1, FOX_HEADS):
        o = o + o_all[h * t:(h + 1) * t, :]
    oattn_ref[0] = o.astype(BF16)


def _mix_sample(p, state, q, k, v, logft, cache_k, cache_v, cache_logft, wg, ps):
    b, t, _ = p.shape
    past = cache_k.shape[1]
    per = lambda *shape: pl.BlockSpec((1,) + shape, lambda i: (i,) + (0,) * len(shape))
    const = lambda *shape: pl.BlockSpec(shape, lambda i: (0,) * len(shape))
    return pl.pallas_call(
        functools.partial(_mix_sample_kernel, past=past),
        grid=(b,),
        in_specs=[per(t, POOL_WIDTH), per(POOL_STATE, POOL_WIDTH), per(t, FOX_WIDTH), per(t, FOX_WIDTH),
                  per(t, FOX_WIDTH), per(FOX_HEADS, t), per(past, FOX_WIDTH), per(past, FOX_WIDTH),
                  per(FOX_HEADS, past), const(4, POOL_GROUP_DIM, POOL_GROUP_DIM), const(1, POOL_WIDTH)],
        out_specs=[per(t, POOL_WIDTH), per(t, FOX_WIDTH), per(POOL_STATE, POOL_WIDTH)],
        out_shape=[
            jax.ShapeDtypeStruct((b, t, POOL_WIDTH), BF16),
            jax.ShapeDtypeStruct((b, t, FOX_WIDTH), BF16),
            jax.ShapeDtypeStruct((b, POOL_STATE, POOL_WIDTH), F32),
        ],
        scratch_shapes=[pltpu.VMEM((HIST + t, POOL_WIDTH), F32)],
        compiler_params=pltpu.CompilerParams(dimension_semantics=("arbitrary",),
                                             vmem_limit_bytes=VMEM_LIMIT),
        name="mix_sample",
    )(p, state, q, k, v, logft, cache_k, cache_v, cache_logft, wg, ps)


def _merge_ffn_kernel(x_ref, op_ref, oa_ref, gmix_ref, wgl_ref, bg_ref, wbp_ref, wba_ref, wout_ref,
                      gffn_ref, wgate_ref, wup_ref, wdown_ref, gfin_ref, y_ref):
    x = x_ref[...]
    xn = _rms(x, gmix_ref[...]).astype(BF16)
    gates = _sigmoid(jnp.dot(xn, wgl_ref[...], preferred_element_type=F32) + bg_ref[...])
    a = jnp.dot(op_ref[...], wbp_ref[...], preferred_element_type=F32)
    b = jnp.dot(oa_ref[...], wba_ref[...], preferred_element_type=F32)
    merged = gates[:, 0:D_MODEL] * a + gates[:, D_MODEL:2 * D_MODEL] * b
    h = x + jnp.dot(merged.astype(BF16), wout_ref[...], preferred_element_type=F32)
    hn = _rms(h, gffn_ref[...]).astype(BF16)
    half = D_FF // 2
    ffn = None
    for c in range(2):
        cols = slice(c * half, (c + 1) * half)
        gate = jnp.dot(hn, wgate_ref[:, cols], preferred_element_type=F32)
        up = jnp.dot(hn, wup_ref[:, cols], preferred_element_type=F32)
        act = (gate * _sigmoid(gate) * up).astype(BF16)
        part = jnp.dot(act, wdown_ref[cols, :], preferred_element_type=F32)
        ffn = part if ffn is None else ffn + part
    y_ref[...] = _rms(h + ffn, gfin_ref[...])


def _merge_ffn(x, o_pool, o_attn, gmix, w_gl, b_gate, w_bp, w_ba, w_out, gffn, w_gate, w_up, w_down, gfin):
    n = x.shape[0]
    tm = min(TM_FFN, n)
    row = lambda width: pl.BlockSpec((tm, width), lambda i: (i, 0))
    const = lambda *shape: pl.BlockSpec(shape, lambda i: (0,) * len(shape), pipeline_mode=pl.Buffered(1))
    return pl.pallas_call(
        _merge_ffn_kernel,
        grid=(n // tm,),
        in_specs=[row(D_MODEL), row(POOL_WIDTH), row(FOX_WIDTH), const(1, D_MODEL),
                  const(D_MODEL, 2 * D_MODEL), const(1, 2 * D_MODEL), const(POOL_WIDTH, D_MODEL),
                  const(FOX_WIDTH, D_MODEL), const(D_MODEL, D_MODEL), const(1, D_MODEL),
                  const(D_MODEL, D_FF), const(D_MODEL, D_FF), const(D_FF, D_MODEL), const(1, D_MODEL)],
        out_specs=row(D_MODEL),
        out_shape=jax.ShapeDtypeStruct((n, D_MODEL), F32),
        compiler_params=pltpu.CompilerParams(dimension_semantics=("arbitrary",),
                                             vmem_limit_bytes=VMEM_LIMIT),
        name="merge_ffn",
    )(x, o_pool, o_attn, gmix, w_gl, b_gate, w_bp, w_ba, w_out, gffn, w_gate, w_up, w_down, gfin)


def kernel(x_prompt, x_sample, cache_k, cache_v, cache_logf, state_pool, norm_mix, w_in, b_forget,
           w_pool_group, pool_scale, w_branch_pool, w_branch_attn, b_gate, w_out, norm_ffn,
           w_ffn_gate, w_ffn_up, w_ffn_down, norm_final):
    depth = w_in.shape[0]
    assert depth == 1, "single-layer trunk"
    bp, sp, _ = x_prompt.shape
    bs, ts, _ = x_sample.shape
    assert bp == 1
    past = cache_k.shape[2]
    l = 0

    w_in_b = w_in[l].astype(BF16)
    n_main = POOL_WIDTH + 3 * FOX_WIDTH
    w_main = w_in_b[:, :n_main]
    w_fl8 = w_in_b[:, n_main:n_main + FOX_HEADS]
    w_fl = jnp.pad(w_fl8, ((0, 0), (0, LANES - FOX_HEADS)))
    w_flt = jnp.pad(w_fl8.T, ((0, 16 - FOX_HEADS), (0, 0)))
    w_gl = w_in_b[:, n_main + FOX_HEADS:]
    bf = b_forget[l].reshape(1, FOX_HEADS)
    bft = jnp.broadcast_to(b_forget[l].reshape(FOX_HEADS, 1), (FOX_HEADS, LANES))
    gmix = norm_mix[l].reshape(1, D_MODEL)
    gffn = norm_ffn[l].reshape(1, D_MODEL)
    gfin = norm_final.reshape(1, D_MODEL)
    wg = w_pool_group[l].astype(BF16)
    ps = pool_scale[l].reshape(1, POOL_WIDTH)
    tail = (gmix, w_gl, b_gate[l].reshape(1, 2 * D_MODEL), w_branch_pool[l].astype(BF16),
            w_branch_attn[l].astype(BF16), w_out[l].astype(BF16), gffn, w_ffn_gate[l].astype(BF16),
            w_ffn_up[l].astype(BF16), w_ffn_down[l].astype(BF16), gfin)

    xp = x_prompt.reshape(sp, D_MODEL)
    k_p, v_p, logf_p, qat, ka, vat, opool_p, pstate_p = _inproj_prompt(
        xp, gmix, w_main, w_fl, w_flt, bf, bft, wg, ps)
    oattn_p = _attn_prompt(qat, ka, vat)
    y_p = _merge_ffn(xp, opool_p, oattn_p, *tail)

    xs = x_sample.reshape(bs * ts, D_MODEL)
    p_s, q_s, k_s, v_s, logf_s, logft_s = _inproj_sample(xs, gmix, w_main, w_fl, w_flt, bf, bft)
    logft_s3 = logft_s.reshape(FOX_HEADS, bs, ts).transpose(1, 0, 2)
    clft = cache_logf[l].transpose(0, 2, 1)
    opool_s, oattn_s, pstate_s = _mix_sample(
        p_s.reshape(bs, ts, POOL_WIDTH), state_pool[l], q_s.reshape(bs, ts, FOX_WIDTH),
        k_s.reshape(bs, ts, FOX_WIDTH), v_s.reshape(bs, ts, FOX_WIDTH), logft_s3,
        cache_k[l].reshape(bs, past, FOX_WIDTH), cache_v[l].reshape(bs, past, FOX_WIDTH), clft, wg, ps)
    y_s = _merge_ffn(xs, opool_s.reshape(bs * ts, POOL_WIDTH), oattn_s.reshape(bs * ts, FOX_WIDTH), *tail)

    return (y_p.reshape(bp, sp, D_MODEL), y_s.reshape(bs, ts, D_MODEL),
            k_p.reshape(1, bp, sp, FOX_HEADS, HEAD_DIM), v_p.reshape(1, bp, sp, FOX_HEADS, HEAD_DIM),
            logf_p.reshape(1, bp, sp, FOX_HEADS), pstate_p[HIST - POOL_STATE:].reshape(1, bp, POOL_STATE, POOL_WIDTH),
            k_s.reshape(1, bs, ts, FOX_HEADS, HEAD_DIM), v_s.reshape(1, bs, ts, FOX_HEADS, HEAD_DIM),
            logf_s.reshape(1, bs, ts, FOX_HEADS), pstate_s.reshape(1, bs, POOL_STATE, POOL_WIDTH))
```
